```python
import jax, jax.numpy as jnp
from jax import lax
import numpy as np

D_MODEL = 1024
BATCH = 32
SEQ = 2048
DEPTH = 1

CTX_LEN = 256
GRID_W = 64

A_HEADS = 4
A_DK = 128
A_DV = 128
A_KW = A_HEADS * A_DK
A_VW = A_HEADS * A_DV
A_CHUNK = 64

B_GROUPS = 4
B_GW = 128
B_W = B_GROUPS * B_GW
B_CHUNK = 2 * GRID_W

D_FF = -(-8 * D_MODEL // (3 * 256)) * 256

N_MOD = 6
EPS = 1e-6

CTX_STATE_COLS = 2 * A_KW + A_VW
SPLIT_IDX = (A_KW, 2 * A_KW, 2 * A_KW + A_VW, 3 * A_KW + A_VW, 3 * A_KW + 2 * A_VW,
             3 * A_KW + 2 * A_VW + B_W, 3 * A_KW + 2 * A_VW + 2 * B_W,
             3 * A_KW + 2 * A_VW + 2 * B_W + D_MODEL)
IN_COLS = 3 * A_KW + 2 * A_VW + 2 * B_W + 2 * D_MODEL

kernel_name = 'hybrid_hgrn2_chunkmlp_dit_block'


def rmsnorm(x, g):
    xf = x.astype(jnp.float32)
    y = xf * lax.rsqrt(jnp.mean(xf * xf, axis=-1, keepdims=True) + EPS)
    return (y * g.astype(jnp.float32)).astype(x.dtype)


def layernorm(x, g, b):
    xf = x.astype(jnp.float32)
    mu = jnp.mean(xf, axis=-1, keepdims=True)
    var = jnp.mean(jnp.square(xf - mu), axis=-1, keepdims=True)
    y = (xf - mu) * lax.rsqrt(var + EPS) * g.astype(jnp.float32) + b.astype(jnp.float32)
    return y.astype(x.dtype)


def modulate(h, shift, scale):
    return h * (1 + scale) + shift


def split_heads(t):
    return t.reshape(t.shape[0], t.shape[1], A_HEADS, -1).astype(jnp.float32)


def flip_seq(t):
    return jnp.flip(t, axis=1)


def forget_gate(f_logit, lb):
    f = lb + (1 - lb) * jax.nn.sigmoid(f_logit.astype(jnp.float32))
    return jnp.log(f), 1 - f


def hgrn2_chunked(q, k, logf, v, s0):
    b_, L = q.shape[:2]
    n = L // A_CHUNK
    rs = lambda t: t.reshape(b_, n, A_CHUNK, *t.shape[2:])
    q, k, logf, v = rs(q), rs(k), rs(logf), rs(v)
    bcum = jnp.cumsum(logf, axis=2)
    blast = bcum[:, :, -1:]
    mid = 0.5 * blast
    q_in = q * jnp.exp(bcum - mid)
    k_in = k * jnp.exp(mid - bcum)
    scores = jnp.einsum('bnthd,bnshd->bnhts', q_in, k_in)
    mask = jnp.tril(jnp.ones((A_CHUNK, A_CHUNK), dtype=bool))
    scores = jnp.where(mask, scores, 0.0)
    o_intra = jnp.einsum('bnhts,bnshe->bnthe', scores, v)
    d_state = jnp.einsum('bnshd,bnshe->bnhde', k * jnp.exp(blast - bcum), v)
    decay = jnp.exp(blast[:, :, 0])

    def step(s, inp):
        a, d = inp
        return a[..., None] * s + d, s

    s_final, s_in = lax.scan(step, s0, (jnp.moveaxis(decay, 1, 0), jnp.moveaxis(d_state, 1, 0)))
    s_in = jnp.moveaxis(s_in, 0, 1)
    o_inter = jnp.einsum('bnthd,bnhde->bnthe', q * jnp.exp(bcum), s_in)
    o = (o_intra + o_inter).reshape(b_, L, A_HEADS, A_DV)
    return o, s_final


def hgrn2_final_state(k, logf, v):
    bcum = jnp.cumsum(logf, axis=1)
    w = k * jnp.exp(bcum[:, -1:] - bcum)
    return jnp.einsum('blhd,blhe->bhde', w, v)


def chunk_sgu(u, v, ln_g, ln_b, w_s, b_s, n_chunks):
    b_, L = u.shape[:2]
    v = layernorm(v, ln_g, ln_b)
    vc = v.reshape(b_, n_chunks, B_CHUNK, B_GROUPS, B_GW)
    mixed = jnp.einsum('gts,bnsgc->bntgc', w_s, vc) + jnp.transpose(b_s)[:, :, None]
    return u * mixed.reshape(b_, L, B_W)


def token_mixers(p, lb, g_norm_a, ln_v_g, ln_v_b, w_s, b_s, w_pa, w_pb, w_o, s0_f, s0_b, n_chunks_b):
    f_f, f_b, i_in, q, og, u, v, ga, gb = jnp.split(p, SPLIT_IDX, axis=-1)
    b_, L = p.shape[:2]
    logf_f, k_f = forget_gate(f_f, lb[0])
    logf_b, k_b = forget_gate(f_b, lb[1])
    qh, vh = split_heads(q), split_heads(i_in)
    o_f, s_f = hgrn2_chunked(qh, split_heads(k_f), split_heads(logf_f), vh, s0_f)
    o_b, s_b = hgrn2_chunked(flip_seq(qh), flip_seq(split_heads(k_b)), flip_seq(split_heads(logf_b)),
                             flip_seq(vh), s0_b)
    o_a = rmsnorm(o_f + flip_seq(o_b), g_norm_a).reshape(b_, L, A_VW).astype(p.dtype)
    o_a = o_a * jax.nn.silu(og)
    o_b_mlp = chunk_sgu(jax.nn.gelu(u), jax.nn.gelu(v), ln_v_g, ln_v_b, w_s, b_s, n_chunks_b)
    merged = jax.nn.sigmoid(ga) * (o_a @ w_pa) + jax.nn.sigmoid(gb) * (o_b_mlp @ w_pb)
    return merged @ w_o, s_f, s_b


def swiglu(h, w_up, w_down):
    a, b = jnp.split(h @ w_up, 2, axis=-1)
    return (jax.nn.silu(a) * b) @ w_down


def setup_inputs(seed: int = 0) -> dict:
    key = jax.random.key(seed)
    ks = jax.random.split(key, 21)
    nrm = lambda k, shape, s: jax.random.normal(k, shape, jnp.float32) * s
    return {
        'x': nrm(ks[0], (BATCH, SEQ, D_MODEL), 1.0),
        'c': nrm(ks[1], (BATCH, D_MODEL), 1.0),
        'ctx': nrm(ks[2], (BATCH, CTX_LEN, D_MODEL), 1.0),
        'c_ctx': nrm(ks[3], (D_MODEL,), 1.0),
        'w_mod': nrm(ks[4], (DEPTH, D_MODEL, N_MOD * D_MODEL), 0.5 * D_MODEL ** -0.5),
        'b_mod': nrm(ks[5], (DEPTH, N_MOD * D_MODEL), 0.01),
        'g_mix': 1.0 + nrm(ks[6], (DEPTH, D_MODEL), 0.1),
        'g_ffn': 1.0 + nrm(ks[7], (DEPTH, D_MODEL), 0.1),
        'w_in': nrm(ks[8], (DEPTH, D_MODEL, IN_COLS), D_MODEL ** -0.5),
        'lb_gamma': nrm(ks[9], (DEPTH + 1, 2, A_KW), 0.5),
        'g_norm_a': 1.0 + nrm(ks[10], (DEPTH, A_DV), 0.1),
        'ln_v_g': 1.0 + nrm(ks[11], (DEPTH, B_W), 0.1),
        'ln_v_b': nrm(ks[12], (DEPTH, B_W), 0.02),
        'w_s': nrm(ks[13], (DEPTH, B_GROUPS, B_CHUNK, B_CHUNK), 0.5 * B_CHUNK ** -0.5),
        'b_s': 1.0 + nrm(ks[14], (DEPTH, B_GROUPS, B_CHUNK), 0.1),
        'w_pa': nrm(ks[15], (DEPTH, A_VW, D_MODEL), A_VW ** -0.5),
        'w_pb': nrm(ks[16], (DEPTH, B_W, D_MODEL), B_W ** -0.5),
        'w_o': nrm(ks[17], (DEPTH, D_MODEL, D_MODEL), D_MODEL ** -0.5),
        'w_up': nrm(ks[18], (DEPTH, D_MODEL, 2 * D_FF), D_MODEL ** -0.5),
        'w_down': nrm(ks[19], (DEPTH, D_FF, D_MODEL), D_FF ** -0.5),
        'g_final': 1.0 + nrm(ks[20], (D_MODEL,), 0.1),
    }


def reference(x, c, ctx, c_ctx, w_mod, b_mod, g_mix, g_ffn, w_in, lb_gamma, g_norm_a,
              ln_v_g, ln_v_b, w_s, b_s, w_pa, w_pb, w_o, w_up, w_down, g_final):
    bsz, L = x.shape[0], x.shape[1]
    rows = L // GRID_W
    n_chunks_lat = rows // 2
    n_chunks_ctx = ctx.shape[1] // B_CHUNK
    lb_all = jnp.cumsum(jax.nn.softmax(lb_gamma.astype(jnp.float32), axis=0), axis=0)
    for l in range(DEPTH):
        last = l == DEPTH - 1
        lb = lb_all[l]
        mod = (jax.nn.silu(c) @ w_mod[l] + b_mod[l]).reshape(bsz, N_MOD, D_MODEL)
        mc = (jax.nn.silu(c_ctx) @ w_mod[l] + b_mod[l]).reshape(N_MOD, D_MODEL)
        hc = modulate(rmsnorm(ctx, g_mix[l]), mc[0], mc[1])
        if last:
            pc = hc @ w_in[l][:, :CTX_STATE_COLS]
            f_f, f_b, i_c = jnp.split(pc, (A_KW, 2 * A_KW), axis=-1)
            logf_f, k_f = forget_gate(f_f, lb[0])
            logf_b, k_b = forget_gate(f_b, lb[1])
            vh = split_heads(i_c)
            s_ctx_f = hgrn2_final_state(split_heads(k_f), split_heads(logf_f), vh)
            s_ctx_b = hgrn2_final_state(flip_seq(split_heads(k_b)), flip_seq(split_heads(logf_b)), flip_seq(vh))
        else:
            zeros = jnp.zeros((bsz, A_HEADS, A_DK, A_DV), jnp.float32)
            mix_c, s_ctx_f, s_ctx_b = token_mixers(hc @ w_in[l], lb, g_norm_a[l], ln_v_g[l], ln_v_b[l],
                                                   w_s[l], b_s[l], w_pa[l], w_pb[l], w_o[l],
                                                   zeros, zeros, n_chunks_ctx)
            ctx = ctx + mc[2] * mix_c
            hc2 = modulate(rmsnorm(ctx, g_ffn[l]), mc[3], mc[4])
            ctx = ctx + mc[5] * swiglu(hc2, w_up[l], w_down[l])
        h = modulate(rmsnorm(x, g_mix[l]), mod[:, 0, None], mod[:, 1, None])
        mix_x, _, _ = token_mixers(h @ w_in[l], lb, g_norm_a[l], ln_v_g[l], ln_v_b[l],
                                   w_s[l], b_s[l], w_pa[l], w_pb[l], w_o[l],
                                   s_ctx_f, s_ctx_b, n_chunks_lat)
        x = x + mod[:, 2, None] * mix_x
        h2 = modulate(rmsnorm(x, g_ffn[l]), mod[:, 3, None], mod[:, 4, None])
        x = x + mod[:, 5, None] * swiglu(h2, w_up[l], w_down[l])
    return rmsnorm(x, g_final)
```

```python
import functools

import jax
import jax.numpy as jnp
from jax import lax
from jax.experimental import pallas as pl
from jax.experimental.pallas import tpu as pltpu

F32 = jnp.float32
BF16 = jnp.bfloat16
EPS = 1e-6

HEADS = 4
DK = 128
DV = 128
KW = HEADS * DK
VW = HEADS * DV
CHUNK = 64
CHUNK_SHIFT = 6
GROUPS = 4
GW = 128
BW = GROUPS * GW
SGU_T = 128
N_MOD = 6
MOD_ROWS = 8

TM = 256
TF = 512
FF_CHUNK = 256

VMEM_LIMIT_BYTES = 60 * 1024 * 1024

C_FF, C_I, C_Q, C_OG, C_U, C_V, C_GA = (
    0, 2 * KW, 2 * KW + VW, 3 * KW + VW, 3 * KW + 2 * VW,
    3 * KW + 2 * VW + BW, 3 * KW + 2 * VW + 2 * BW)


def _dot(a, b):
    return jnp.dot(a, b, preferred_element_type=F32)


def _dot_nt(a, b):
    return lax.dot_general(a, b, (((1,), (1,)), ((), ())), preferred_element_type=F32)


def _dot_tn(a, b):
    return lax.dot_general(a, b, (((0,), (0,)), ((), ())), preferred_element_type=F32)


def _rmsnorm(x, g):
    return x * lax.rsqrt(jnp.mean(x * x, axis=-1, keepdims=True) + EPS) * g


def _modulated_norm(x, g, shift, scale):
    return _rmsnorm(x, g) * (1 + scale) + shift


def _lower_bound(lbg_ref):
    g0 = lbg_ref[0:1, :]
    g1 = lbg_ref[1:2, :]
    m = jnp.maximum(g0, g1)
    e0 = jnp.exp(g0 - m)
    e1 = jnp.exp(g1 - m)
    return e0 / (e0 + e1)


def _forget_gate(z, lb):
    f = lb + (1 - lb) * jax.nn.sigmoid(z)
    return jnp.log(f), 1 - f


def _split3(a):
    hi = a.astype(BF16)
    r = a - hi.astype(F32)
    mid = r.astype(BF16)
    lo = (r - mid.astype(F32)).astype(BF16)
    return hi, mid, lo


def _masked_sum(mask_bf16, a):
    n = a.shape[1]
    parts = jnp.concatenate(_split3(a), axis=1)
    s = _dot(mask_bf16, parts)
    return s[:, :n] + s[:, n:2 * n] + s[:, 2 * n:]


def _chunk_masks(t):
    r = lax.broadcasted_iota(jnp.int32, (t, t), 0)
    c = lax.broadcasted_iota(jnp.int32, (t, t), 1)
    same = lax.shift_right_logical(r, CHUNK_SHIFT) == lax.shift_right_logical(c, CHUNK_SHIFT)
    return same & (c <= r), same & (c >= r)


def _mod_kernel(c_ref, w_ref, b_ref, o_ref):
    a = jax.nn.silu(c_ref[...]).astype(BF16)
    o_ref[...] = _dot(a, w_ref[...].astype(BF16)) + b_ref[...]


def _mod_call(c_all, w_mod, b_mod):
    rows, d = c_all.shape
    n = w_mod.shape[1]
    return pl.pallas_call(
        _mod_kernel,
        grid=(n // d,),
        in_specs=[
            pl.BlockSpec((rows, d), lambda j: (0, 0)),
            pl.BlockSpec((d, d), lambda j: (0, j)),
            pl.BlockSpec((1, d), lambda j: (0, j)),
        ],
        out_specs=pl.BlockSpec((rows, d), lambda j: (0, j)),
        out_shape=jax.ShapeDtypeStruct((rows, n), F32),
        compiler_params=pltpu.CompilerParams(dimension_semantics=("arbitrary",)),
        name="mod",
    )(c_all, w_mod, b_mod)


def _ctx_kernel(ctx_ref, mc_ref, g_ref, lbg_ref, w_ref, s_ref):
    lc = ctx_ref.shape[1]
    hc = _modulated_norm(ctx_ref[0], g_ref[...], mc_ref[0:1, :], mc_ref[1:2, :]).astype(BF16)
    pc = _dot(hc, w_ref[...])
    logf, k = _forget_gate(pc[:, :2 * KW], _lower_bound(lbg_ref))
    v = pc[:, 2 * KW:].astype(BF16)
    r = lax.broadcasted_iota(jnp.int32, (lc, lc), 0)
    c = lax.broadcasted_iota(jnp.int32, (lc, lc), 1)
    after = (c > r).astype(BF16)
    before = (c < r).astype(BF16)
    w_f = (k[:, :KW] * jnp.exp(_masked_sum(after, logf[:, :KW]))).astype(BF16)
    w_b = (k[:, KW:] * jnp.exp(_masked_sum(before, logf[:, KW:]))).astype(BF16)
    for h in range(HEADS):
        vh = v[:, h * DV:(h + 1) * DV]
        s_ref[0, 0, h] = _dot_tn(vh, w_f[:, h * DK:(h + 1) * DK])
        s_ref[0, 1, h] = _dot_tn(vh, w_b[:, h * DK:(h + 1) * DK])


def _ctx_call(ctx, mc, g_mix, lbg, w_in):
    b, lc, d = ctx.shape
    ncol = 2 * KW + VW
    return pl.pallas_call(
        _ctx_kernel,
        grid=(b,),
        in_specs=[
            pl.BlockSpec((1, lc, d), lambda i: (i, 0, 0)),
            pl.BlockSpec((MOD_ROWS, d), lambda i: (0, 0)),
            pl.BlockSpec((1, d), lambda i: (0, 0)),
            pl.BlockSpec((2, 2 * KW), lambda i: (0, 0)),
            pl.BlockSpec((d, ncol), lambda i: (0, 0)),
        ],
        out_specs=pl.BlockSpec((1, 2, HEADS, DV, DK), lambda i: (i, 0, 0, 0, 0)),
        out_shape=jax.ShapeDtypeStruct((b, 2, HEADS, DV, DK), F32),
        compiler_params=pltpu.CompilerParams(
            dimension_semantics=("arbitrary",), vmem_limit_bytes=VMEM_LIMIT_BYTES),
        name="ctx_state",
    )(ctx, mc, g_mix, lbg, w_in)


def _mixer_kernel(x_ref, mod_ref, s0_ref, gmix_ref, lbg_ref, gna_ref, lng_ref, lnb_ref,
                  bs_ref, ws_ref, win_ref, wpa_ref, wpb_ref, wo_ref,
                  out_ref,
                  stf_ref, stb_ref, opart_ref, qdecb_ref, kdecb_ref, vb_ref, decayb_ref,
                  *, n_tiles):
    s = pl.program_id(1)
    n_chunks = TM // CHUNK

    @pl.when(s == 0)
    def _():
        stf_ref[...] = s0_ref[0, 0]
        stb_ref[...] = s0_ref[0, 1]

    def hidden():
        return _modulated_norm(x_ref[0], gmix_ref[...], mod_ref[0, 0:1, :],
                               mod_ref[0, 1:2, :]).astype(BF16)

    @pl.when(s < n_tiles)
    def _():
        row0 = pl.multiple_of(s * TM, TM)
        h = hidden()
        logf, k = _forget_gate(_dot(h, win_ref[:, C_FF:C_I]), _lower_bound(lbg_ref))
        v = _dot(h, win_ref[:, C_I:C_Q]).astype(BF16)
        q = _dot(h, win_ref[:, C_Q:C_OG])
        tril, triu = _chunk_masks(TM)

        def direction(d):
            cols = slice(d * KW, (d + 1) * KW)
            mask = tril if d == 0 else triu
            bc = _masked_sum(mask.astype(BF16), logf[:, cols])
            edge = CHUNK - 1 if d == 0 else 0
            blast = [bc[c * CHUNK + edge:c * CHUNK + edge + 1, :] for c in range(n_chunks)]
            full = lambda rows: jnp.concatenate(
                [jnp.broadcast_to(r, (CHUNK, KW)) for r in rows], axis=0)
            mid = full([0.5 * r for r in blast])
            emid = full([jnp.exp(0.5 * r) for r in blast])
            q_in = q * jnp.exp(bc - mid)
            k_in = k[:, cols] * jnp.exp(mid - bc)
            qdec = (q_in * emid).astype(BF16)
            kdec = (k_in * emid).astype(BF16)
            q_in = q_in.astype(BF16)
            k_in = k_in.astype(BF16)
            decay = [jnp.exp(r) for r in blast]
            intra = []
            for hh in range(HEADS):
                hs = slice(hh * DK, (hh + 1) * DK)
                sc = jnp.where(mask, _dot_nt(q_in[:, hs], k_in[:, hs]), 0.0).astype(BF16)
                intra.append(_dot(sc, v[:, hh * DV:(hh + 1) * DV]))
            return jnp.concatenate(intra, axis=1), qdec, kdec, decay

        o_f, qdec_f, kdec_f, decay_f = direction(0)
        o_b, qdec_b, kdec_b, decay_b = direction(1)

        inter = []
        for c in range(n_chunks):
            rows = slice(c * CHUNK, (c + 1) * CHUNK)
            per_head = []
            for hh in range(HEADS):
                hs = slice(hh * DK, (hh + 1) * DK)
                st = stf_ref[hh]
                per_head.append(_dot_nt(qdec_f[rows, hs], st.astype(BF16)))
                d_st = _dot_tn(v[rows, hh * DV:(hh + 1) * DV], kdec_f[rows, hs])
                stf_ref[hh] = st * decay_f[c][:, hs] + d_st
            inter.append(jnp.concatenate(per_head, axis=1))
        opart_ref[pl.ds(row0, TM), :] = o_f + o_b + jnp.concatenate(inter, axis=0)
        qdecb_ref[pl.ds(row0, TM), :] = qdec_b
        kdecb_ref[pl.ds(row0, TM), :] = kdec_b
        vb_ref[pl.ds(row0, TM), :] = v
        decayb_ref[s] = jnp.concatenate(
            decay_b + [jnp.zeros((MOD_ROWS - n_chunks, KW), F32)], axis=0)

    @pl.when(s >= n_tiles)
    def _():
        t = 2 * n_tiles - 1 - s
        row0 = pl.multiple_of(t * TM, TM)
        x = x_ref[0]
        h = hidden()
        og = _dot(h, win_ref[:, C_OG:C_U])
        u = _dot(h, win_ref[:, C_U:C_V])
        vv = _dot(h, win_ref[:, C_V:C_GA])
        ga = _dot(h, win_ref[:, C_GA:C_GA + x.shape[1]])
        gb = _dot(h, win_ref[:, C_GA + x.shape[1]:])

        decay_b = decayb_ref[t]
        inter = [None] * n_chunks
        for c in reversed(range(n_chunks)):
            rows = pl.ds(row0 + c * CHUNK, CHUNK)
            per_head = []
            for hh in range(HEADS):
                hs = slice(hh * DK, (hh + 1) * DK)
                st = stb_ref[hh]
                per_head.append(_dot_nt(qdecb_ref[rows, hs], st.astype(BF16)))
                d_st = _dot_tn(vb_ref[rows, hh * DV:(hh + 1) * DV], kdecb_ref[rows, hs])
                stb_ref[hh] = st * decay_b[c:c + 1, hs] + d_st
            inter[c] = jnp.concatenate(per_head, axis=1)
        o = opart_ref[pl.ds(row0, TM), :] + jnp.concatenate(inter, axis=0)
        o_a = jnp.concatenate(
            [_rmsnorm(o[:, hh * DV:(hh + 1) * DV], gna_ref[...]) for hh in range(HEADS)], axis=1)
        o_a = (o_a * jax.nn.silu(og)).astype(BF16)
        y_a = _dot(o_a, wpa_ref[...])

        vg = jax.nn.gelu(vv)
        mu = jnp.mean(vg, axis=-1, keepdims=True)
        var = jnp.mean(jnp.square(vg - mu), axis=-1, keepdims=True)
        vn = ((vg - mu) * lax.rsqrt(var + EPS) * lng_ref[...] + lnb_ref[...]).astype(BF16)
        mixed = []
        for j in range(TM // SGU_T):
            rows = slice(j * SGU_T, (j + 1) * SGU_T)
            mixed.append(jnp.concatenate(
                [_dot(ws_ref[g], vn[rows, g * GW:(g + 1) * GW]) for g in range(GROUPS)],
                axis=1) + bs_ref[...])
        o_m = (jax.nn.gelu(u) * jnp.concatenate(mixed, axis=0)).astype(BF16)
        y_b = _dot(o_m, wpb_ref[...])

        merged = (jax.nn.sigmoid(ga) * y_a + jax.nn.sigmoid(gb) * y_b).astype(BF16)
        out_ref[0] = x + mod_ref[0, 2:3, :] * _dot(merged, wo_ref[...])


def _mixer_call(x, mod, s0, g_mix, lbg, g_norm_a, ln_g, ln_b, bs_full, w_s, w_in, w_pa, w_pb, w_o):
    b, l, d = x.shape
    n_tiles = l // TM
    const = lambda shape: pl.BlockSpec(shape, lambda i, s: (0,) * len(shape))
    tile = lambda i, s: (i, jnp.where(s < n_tiles, s, 2 * n_tiles - 1 - s), 0)
    out_tile = lambda i, s: (i, jnp.where(s < n_tiles, n_tiles - 1, 2 * n_tiles - 1 - s), 0)
    return pl.pallas_call(
        functools.partial(_mixer_kernel, n_tiles=n_tiles),
        grid=(b, 2 * n_tiles),
        in_specs=[
            pl.BlockSpec((1, TM, d), tile),
            pl.BlockSpec((1, MOD_ROWS, d), lambda i, s: (i, 0, 0)),
            pl.BlockSpec((1, 2, HEADS, DV, DK), lambda i, s: (i, 0, 0, 0, 0)),
            const((1, d)),
            const((2, 2 * KW)),
            const((1, DV)),
            const((1, BW)),
            const((1, BW)),
            const((SGU_T, BW)),
            const((GROUPS, SGU_T, SGU_T)),
            const(w_in.shape),
            const(w_pa.shape),
            const(w_pb.shape),
            const(w_o.shape),
        ],
        out_specs=pl.BlockSpec((1, TM, d), out_tile),
        out_shape=jax.ShapeDtypeStruct((b, l, d), F32),
        scratch_shapes=[
            pltpu.VMEM((HEADS, DV, DK), F32),
            pltpu.VMEM((HEADS, DV, DK), F32),
            pltpu.VMEM((l, VW), F32),
            pltpu.VMEM((l, KW), BF16),
            pltpu.VMEM((l, KW), BF16),
            pltpu.VMEM((l, VW), BF16),
            pltpu.VMEM((n_tiles, MOD_ROWS, KW), F32),
        ],
        compiler_params=pltpu.CompilerParams(
            dimension_semantics=("arbitrary", "arbitrary"), vmem_limit_bytes=VMEM_LIMIT_BYTES),
        name="mixer",
    )(x, mod, s0, g_mix, lbg, g_norm_a, ln_g, ln_b, bs_full, w_s, w_in, w_pa, w_pb, w_o)


def _ffn_kernel(x_ref, mod_ref, gffn_ref, gfin_ref, wup_ref, wdown_ref, out_ref):
    x = x_ref[0]
    h = _modulated_norm(x, gffn_ref[...], mod_ref[0, 3:4, :], mod_ref[0, 4:5, :]).astype(BF16)
    acc = jnp.zeros(x.shape, F32)
    for j in range(wdown_ref.shape[0] // FF_CHUNK):
        ab = _dot(h, wup_ref[:, 2 * j * FF_CHUNK:2 * (j + 1) * FF_CHUNK])
        g = (jax.nn.silu(ab[:, :FF_CHUNK]) * ab[:, FF_CHUNK:]).astype(BF16)
        acc = acc + _dot(g, wdown_ref[j * FF_CHUNK:(j + 1) * FF_CHUNK, :])
    out_ref[0] = _rmsnorm(x + mod_ref[0, 5:6, :] * acc, gfin_ref[...])


def _ffn_call(x, mod, g_ffn, g_final, w_up, w_down):
    b, l, d = x.shape
    const = lambda shape: pl.BlockSpec(shape, lambda i, t: (0,) * len(shape))
    return pl.pallas_call(
        _ffn_kernel,
        grid=(b, l // TF),
        in_specs=[
            pl.BlockSpec((1, TF, d), lambda i, t: (i, t, 0)),
            pl.BlockSpec((1, MOD_ROWS, d), lambda i, t: (i, 0, 0)),
            const((1, d)),
            const((1, d)),
            const(w_up.shape),
            const(w_down.shape),
        ],
        out_specs=pl.BlockSpec((1, TF, d), lambda i, t: (i, t, 0)),
        out_shape=jax.ShapeDtypeStruct((b, l, d), F32),
        compiler_params=pltpu.CompilerParams(
            dimension_semantics=("arbitrary", "arbitrary"), vmem_limit_bytes=VMEM_LIMIT_BYTES),
        name="ffn",
    )(x, mod, g_ffn, g_final, w_up, w_down)


def kernel(x, c, ctx, c_ctx, w_mod, b_mod, g_mix, g_ffn, w_in, lb_gamma, g_norm_a, ln_v_g, ln_v_b,
           w_s, b_s, w_pa, w_pb, w_o, w_up, w_down, g_final):
    b, l, d = x.shape
    assert w_mod.shape[0] == 1 and lb_gamma.shape[0] == 2, "single-layer block only"
    assert l % TM == 0 and l % TF == 0 and TM % SGU_T == 0 and SGU_T % CHUNK == 0
    assert (1 << CHUNK_SHIFT) == CHUNK and ctx.shape[1] % 8 == 0
    d_ff = w_down.shape[1]
    assert d_ff % FF_CHUNK == 0 and w_in.shape[2] == C_GA + 2 * d

    pad_rows = -(b + 1) % 8
    c_all = jnp.concatenate([c, c_ctx[None, :], jnp.zeros((pad_rows, d), F32)], axis=0)
    mod_all = _mod_call(c_all, w_mod[0], b_mod)
    pad_mod = lambda m: jnp.pad(m, [(0, 0)] * (m.ndim - 2) + [(0, MOD_ROWS - N_MOD), (0, 0)])
    mod = pad_mod(mod_all[:b].reshape(b, N_MOD, d))
    mc = pad_mod(mod_all[b].reshape(N_MOD, d))

    w_in_bf = w_in[0].astype(BF16)
    lbg = lb_gamma.reshape(2, 2 * KW)
    s0 = _ctx_call(ctx, mc, g_mix, lbg, w_in_bf)

    bs_full = jnp.repeat(jnp.transpose(b_s[0]), GW, axis=1)
    x1 = _mixer_call(x, mod, s0, g_mix, lbg, g_norm_a, ln_v_g, ln_v_b, bs_full,
                     w_s[0].astype(BF16), w_in_bf, w_pa[0].astype(BF16), w_pb[0].astype(BF16),
                     w_o[0].astype(BF16))

    n_ff = d_ff // FF_CHUNK
    w_up_r = w_up[0].reshape(d, 2, n_ff, FF_CHUNK).transpose(0, 2, 1, 3).reshape(d, 2 * d_ff)
    return _ffn_call(x1, mod, g_ffn, g_final[None, :], w_up_r.astype(BF16),
                     w_down[0].astype(BF16))
```

```python
import functools

import jax
import jax.numpy as jnp
from jax import lax
from jax.experimental import pallas as pl
from jax.experimental.pallas import tpu as pltpu

F32 = jnp.float32
BF16 = jnp.bfloat16
EPS = 1e-6

HEADS = 4
DK = 128
DV = 128
KW = HEADS * DK
VW = HEADS * DV
CHUNK = 64
CHUNK_SHIFT = 6
GROUPS = 4
GW = 128
BW = GROUPS * GW
SGU_T = 128
N_MOD = 6
MOD_ROWS = 8

TM = 512
SUB = 256
TF = 512
OUT_BLOCK = 256
FF_CHUNK = 256

VMEM_LIMIT_BYTES = 60 * 1024 * 1024

C_FF, C_I, C_Q, C_OG, C_U, C_V, C_GA = (
    0, 2 * KW, 2 * KW + VW, 3 * KW + VW, 3 * KW + 2 * VW,
    3 * KW + 2 * VW + BW, 3 * KW + 2 * VW + 2 * BW)


def _dot(a, b):
    return jnp.dot(a, b, preferred_element_type=F32)


def _dot_nt(a, b):
    return lax.dot_general(a, b, (((1,), (1,)), ((), ())), preferred_element_type=F32)


def _dot_tn(a, b):
    return lax.dot_general(a, b, (((0,), (0,)), ((), ())), preferred_element_type=F32)


def _rmsnorm(x, g):
    return x * lax.rsqrt(jnp.mean(x * x, axis=-1, keepdims=True) + EPS) * g


def _modulated_norm(x, g, shift, scale):
    return _rmsnorm(x, g) * (1 + scale) + shift


def _lower_bound(lbg_ref):
    g0 = lbg_ref[0:1, :]
    g1 = lbg_ref[1:2, :]
    m = jnp.maximum(g0, g1)
    e0 = jnp.exp(g0 - m)
    e1 = jnp.exp(g1 - m)
    return e0 / (e0 + e1)


def _forget_gate(z, lb):
    f = lb + (1 - lb) * jax.nn.sigmoid(z)
    return jnp.log(f), 1 - f


def _masked_sum(mask_bf16, a):
    n = a.shape[1]
    hi = a.astype(BF16)
    lo = (a - hi.astype(F32)).astype(BF16)
    s = _dot(mask_bf16, jnp.concatenate([hi, lo], axis=1))
    return s[:, :n] + s[:, n:]


def _chunk_masks(t):
    r = lax.broadcasted_iota(jnp.int32, (t, t), 0)
    c = lax.broadcasted_iota(jnp.int32, (t, t), 1)
    same = lax.shift_right_logical(r, CHUNK_SHIFT) == lax.shift_right_logical(c, CHUNK_SHIFT)
    return same & (c <= r), same & (c >= r)


def _mod_kernel(c_ref, w_ref, b_ref, o_ref):
    a = jax.nn.silu(c_ref[...]).astype(BF16)
    o_ref[...] = _dot(a, w_ref[...].astype(BF16)) + b_ref[...]


def _mod_call(c_all, w_mod, b_mod):
    rows, d = c_all.shape
    n = w_mod.shape[1]
    return pl.pallas_call(
        _mod_kernel,
        grid=(n // d,),
        in_specs=[
            pl.BlockSpec((rows, d), lambda j: (0, 0)),
            pl.BlockSpec((d, d), lambda j: (0, j)),
            pl.BlockSpec((1, d), lambda j: (0, j)),
        ],
        out_specs=pl.BlockSpec((rows, d), lambda j: (0, j)),
        out_shape=jax.ShapeDtypeStruct((rows, n), F32),
        compiler_params=pltpu.CompilerParams(dimension_semantics=("arbitrary",)),
        name="mod",
    )(c_all, w_mod, b_mod)


def _ctx_kernel(ctx_ref, mc_ref, g_ref, lbg_ref, w_ref, s_ref):
    lc = ctx_ref.shape[1]
    hc = _modulated_norm(ctx_ref[0], g_ref[...], mc_ref[0:1, :], mc_ref[1:2, :]).astype(BF16)
    pc = _dot(hc, w_ref[...])
    logf, k = _forget_gate(pc[:, :2 * KW], _lower_bound(lbg_ref))
    v = pc[:, 2 * KW:].astype(BF16)
    r = lax.broadcasted_iota(jnp.int32, (lc, lc), 0)
    c = lax.broadcasted_iota(jnp.int32, (lc, lc), 1)
    after = (c > r).astype(BF16)
    before = (c < r).astype(BF16)
    w_f = (k[:, :KW] * jnp.exp(_masked_sum(after, logf[:, :KW]))).astype(BF16)
    w_b = (k[:, KW:] * jnp.exp(_masked_sum(before, logf[:, KW:]))).astype(BF16)
    for h in range(HEADS):
        vh = v[:, h * DV:(h + 1) * DV]
        s_ref[0, 0, h] = _dot_tn(vh, w_f[:, h * DK:(h + 1) * DK])
        s_ref[0, 1, h] = _dot_tn(vh, w_b[:, h * DK:(h + 1) * DK])


def _ctx_call(ctx, mc, g_mix, lbg, w_in):
    b, lc, d = ctx.shape
    ncol = 2 * KW + VW
    return pl.pallas_call(
        _ctx_kernel,
        grid=(b,),
        in_specs=[
            pl.BlockSpec((1, lc, d), lambda i: (i, 0, 0)),
            pl.BlockSpec((MOD_ROWS, d), lambda i: (0, 0)),
            pl.BlockSpec((1, d), lambda i: (0, 0)),
            pl.BlockSpec((2, 2 * KW), lambda i: (0, 0)),
            pl.BlockSpec((d, ncol), lambda i: (0, 0)),
        ],
        out_specs=pl.BlockSpec((1, 2, HEADS, DV, DK), lambda i: (i, 0, 0, 0, 0)),
        out_shape=jax.ShapeDtypeStruct((b, 2, HEADS, DV, DK), F32),
        compiler_params=pltpu.CompilerParams(
            dimension_semantics=("arbitrary",), vmem_limit_bytes=VMEM_LIMIT_BYTES),
        name="ctx_state",
    )(ctx, mc, g_mix, lbg, w_in)


def _mixer_kernel(x_ref, mod_ref, s0_ref, gmix_ref, lbg_ref, gna_ref, lng_ref, lnb_ref,
                  bs_ref, ws_ref, win_ref, wpa_ref, wpb_ref, wo_ref,
                  out_ref,
                  stf_ref, stb_ref, opart_ref, qdecb_ref, kdecb_ref, vb_ref, decayb_ref, hb_ref,
                  *, n_tiles):
    s = pl.program_id(1)
    n_chunks = SUB // CHUNK
    n_sub = TM // SUB
    chunks = [(j, c) for j in range(n_sub) for c in range(n_chunks)]
    head = lambda hh: slice(hh * DK, (hh + 1) * DK)

    @pl.when(s == 0)
    def _():
        stf_ref[...] = s0_ref[0, 0]
        stb_ref[...] = s0_ref[0, 1]

    def ascend():
        tile0 = pl.multiple_of(s * TM, TM)
        gain = gmix_ref[...] * (1 + mod_ref[0, 1:2, :])
        x = x_ref[0]
        h = (x * lax.rsqrt(jnp.mean(x * x, axis=-1, keepdims=True) + EPS) * gain
             + mod_ref[0, 0:1, :]).astype(BF16)
        hb_ref[pl.ds(tile0, TM), :] = h
        zf = _dot(h, win_ref[:, C_FF:C_I])
        q = _dot(h, win_ref[:, C_Q:C_OG])
        v = _dot(h, win_ref[:, C_I:C_Q]).astype(BF16)
        vb_ref[pl.ds(tile0, TM), :] = v
        logf, k = _forget_gate(zf, _lower_bound(lbg_ref))
        masks = _chunk_masks(SUB)
        mask_bf = (masks[0].astype(BF16), masks[1].astype(BF16))
        full = lambda rs: jnp.concatenate(
            [jnp.broadcast_to(r, (CHUNK, KW)) for r in rs], axis=0)
        probs = [(j, d) for j in range(n_sub) for d in range(2)]
        rows_of = lambda j: slice(j * SUB, (j + 1) * SUB)
        cols_of = lambda d: slice(d * KW, (d + 1) * KW)
        crow = lambda c: slice(c * CHUNK, (c + 1) * CHUNK)
        bc = {p: _masked_sum(mask_bf[p[1]], logf[rows_of(p[0]), cols_of(p[1])]) for p in probs}
        q_in, k_in, qdec, kdec, decay = {}, {}, {}, {}, {}
        for p in probs:
            j, d = p
            edge = CHUNK - 1 if d == 0 else 0
            blast = [bc[p][c * CHUNK + edge:c * CHUNK + edge + 1, :] for c in range(n_chunks)]
            mid = full([0.5 * r for r in blast])
            emid = full([jnp.exp(0.5 * r) for r in blast])
            qi = q[rows_of(j)] * jnp.exp(bc[p] - mid)
            ki = k[rows_of(j), cols_of(d)] * jnp.exp(mid - bc[p])
            qdec[p] = (qi * emid).astype(BF16)
            kdec[p] = (ki * emid).astype(BF16)
            q_in[p] = qi.astype(BF16)
            k_in[p] = ki.astype(BF16)
            decay[p] = [jnp.exp(r) for r in blast]
        sc = {}
        for p in probs:
            for hh in range(HEADS):
                sc[p, hh] = jnp.where(masks[p[1]],
                                      _dot_nt(q_in[p][:, head(hh)], k_in[p][:, head(hh)]),
                                      0.0).astype(BF16)
        intra = {}
        for p in probs:
            intra[p] = jnp.concatenate(
                [_dot(sc[p, hh], v[rows_of(p[0]), hh * DV:(hh + 1) * DV]) for hh in range(HEADS)],
                axis=1)
        d_st = {(j, c, hh): _dot_tn(v[j * SUB + c * CHUNK:j * SUB + (c + 1) * CHUNK,
                                      hh * DV:(hh + 1) * DV], kdec[j, 0][crow(c), head(hh)])
                for (j, c) in chunks for hh in range(HEADS)}
        st = [stf_ref[hh] for hh in range(HEADS)]
        entering = {}
        for (j, c) in chunks:
            for hh in range(HEADS):
                entering[j, c, hh] = st[hh].astype(BF16)
                st[hh] = st[hh] * decay[j, 0][c][:, head(hh)] + d_st[j, c, hh]
        for hh in range(HEADS):
            stf_ref[hh] = st[hh]
        for j in range(n_sub):
            row0 = pl.multiple_of(s * TM + j * SUB, SUB)
            inter = jnp.concatenate(
                [jnp.concatenate([_dot_nt(qdec[j, 0][crow(c), head(hh)], entering[j, c, hh])
                                  for hh in range(HEADS)], axis=1) for c in range(n_chunks)], axis=0)
            opart_ref[pl.ds(row0, SUB), :] = intra[j, 0] + intra[j, 1] + inter
            qdecb_ref[pl.ds(row0, SUB), :] = qdec[j, 1]
            kdecb_ref[pl.ds(row0, SUB), :] = kdec[j, 1]
            decayb_ref[s * n_sub + j] = jnp.concatenate(
                decay[j, 1] + [jnp.zeros((MOD_ROWS - n_chunks, KW), F32)], axis=0)

    def descend(t):
        row0 = pl.multiple_of(t * TM, TM)
        d = x_ref.shape[2]
        h = hb_ref[pl.ds(row0, TM), :]
        vv = _dot(h, win_ref[:, C_V:C_GA])
        u = _dot(h, win_ref[:, C_U:C_V])
        og = _dot(h, win_ref[:, C_OG:C_U])

        vg = jax.nn.gelu(vv)
        mu = jnp.mean(vg, axis=-1, keepdims=True)
        var = jnp.mean(jnp.square(vg - mu), axis=-1, keepdims=True)
        vn = ((vg - mu) * lax.rsqrt(var + EPS) * lng_ref[...] + lnb_ref[...]).astype(BF16)
        mixed = []
        for jj in range(TM // SGU_T):
            rows = slice(jj * SGU_T, (jj + 1) * SGU_T)
            mixed.append(jnp.concatenate(
                [_dot(ws_ref[g], vn[rows, g * GW:(g + 1) * GW]) for g in range(GROUPS)],
                axis=1) + bs_ref[...])

        crows = lambda j, c: pl.ds(row0 + j * SUB + c * CHUNK, CHUNK)
        d_st = {(j, c, hh): _dot_tn(vb_ref[crows(j, c), hh * DV:(hh + 1) * DV],
                                    kdecb_ref[crows(j, c), head(hh)])
                for (j, c) in chunks for hh in range(HEADS)}
        decay_b = [decayb_ref[t * n_sub + j] for j in range(n_sub)]
        st = [stb_ref[hh] for hh in range(HEADS)]
        entering = {}
        for (j, c) in reversed(chunks):
            for hh in range(HEADS):
                entering[j, c, hh] = st[hh].astype(BF16)
                st[hh] = st[hh] * decay_b[j][c:c + 1, head(hh)] + d_st[j, c, hh]
        for hh in range(HEADS):
            stb_ref[hh] = st[hh]
        inter = [jnp.concatenate([_dot_nt(qdecb_ref[crows(j, c), head(hh)], entering[j, c, hh])
                                  for hh in range(HEADS)], axis=1) for (j, c) in chunks]

        o_m = (jax.nn.gelu(u) * jnp.concatenate(mixed, axis=0)).astype(BF16)
        y_b = _dot(o_m, wpb_ref[...])
        o = opart_ref[pl.ds(row0, TM), :] + jnp.concatenate(inter, axis=0)
        o_a = jnp.concatenate(
            [_rmsnorm(o[:, hh * DV:(hh + 1) * DV], gna_ref[...]) for hh in range(HEADS)], axis=1)
        o_a = (o_a * jax.nn.silu(og)).astype(BF16)
        y_a = _dot(o_a, wpa_ref[...])

        mix = jnp.zeros((TM, d), F32)
        for jb in range(d // OUT_BLOCK):
            cb = slice(jb * OUT_BLOCK, (jb + 1) * OUT_BLOCK)
            ga = _dot(h, win_ref[:, C_GA + jb * OUT_BLOCK:C_GA + (jb + 1) * OUT_BLOCK])
            gb = _dot(h, win_ref[:, C_GA + d + jb * OUT_BLOCK:C_GA + d + (jb + 1) * OUT_BLOCK])
            merged = (jax.nn.sigmoid(ga) * y_a[:, cb] + jax.nn.sigmoid(gb) * y_b[:, cb]).astype(BF16)
            mix = mix + _dot(merged, wo_ref[cb, :])
        out_ref[0] = x_ref[0] + mod_ref[0, 2:3, :] * mix

    @pl.when(s < n_tiles)
    def _():
        ascend()

    @pl.when(s >= n_tiles)
    def _():
        descend(2 * n_tiles - 1 - s)


def _mixer_call(x, mod, s0, g_mix, lbg, g_norm_a, ln_g, ln_b, bs_full, w_s, w_in, w_pa, w_pb, w_o):
    b, l, d = x.shape
    n_tiles = l // TM
    const = lambda shape: pl.BlockSpec(shape, lambda i, s: (0,) * len(shape),
                                       pipeline_mode=pl.Buffered(1))
    tile = lambda i, s: (i, jnp.where(s < n_tiles, s, 2 * n_tiles - 1 - s), 0)
    out_tile = lambda i, s: (i, jnp.where(s < n_tiles, n_tiles - 1, 2 * n_tiles - 1 - s), 0)
    return pl.pallas_call(
        functools.partial(_mixer_kernel, n_tiles=n_tiles),
        grid=(b, 2 * n_tiles),
        in_specs=[
            pl.BlockSpec((1, TM, d), tile),
            pl.BlockSpec((1, MOD_ROWS, d), lambda i, s: (i, 0, 0)),
            pl.BlockSpec((1, 2, HEADS, DV, DK), lambda i, s: (i, 0, 0, 0, 0)),
            const((1, d)),
            const((2, 2 * KW)),
            const((1, DV)),
            const((1, BW)),
            const((1, BW)),
            const((SGU_T, BW)),
            const((GROUPS, SGU_T, SGU_T)),
            const(w_in.shape),
            const(w_pa.shape),
            const(w_pb.shape),
            const(w_o.shape),
        ],
        out_specs=pl.BlockSpec((1, TM, d), out_tile),
        out_shape=jax.ShapeDtypeStruct((b, l, d), F32),
        scratch_shapes=[
            pltpu.VMEM((HEADS, DV, DK), F32),
            pltpu.VMEM((HEADS, DV, DK), F32),
            pltpu.VMEM((l, VW), F32),
            pltpu.VMEM((l, KW), BF16),
            pltpu.VMEM((l, KW), BF16),
            pltpu.VMEM((l, VW), BF16),
            pltpu.VMEM((l // SUB, MOD_ROWS, KW), F32),
            pltpu.VMEM((l, d), BF16),
        ],
        compiler_params=pltpu.CompilerParams(
            dimension_semantics=("arbitrary", "arbitrary"), vmem_limit_bytes=VMEM_LIMIT_BYTES),
        name="mixer",
    )(x, mod, s0, g_mix, lbg, g_norm_a, ln_g, ln_b, bs_full, w_s, w_in, w_pa, w_pb, w_o)


def _ffn_kernel(x_ref, mod_ref, gffn_ref, gfin_ref, wup_ref, wdown_ref, out_ref):
    d_ff = wdown_ref.shape[0]
    x = x_ref[0]
    h = _modulated_norm(x, gffn_ref[...], mod_ref[0, 3:4, :], mod_ref[0, 4:5, :]).astype(BF16)
    acc = jnp.zeros(x.shape, F32)
    for j in range(d_ff // FF_CHUNK):
        cols = slice(j * FF_CHUNK, (j + 1) * FF_CHUNK)
        a = _dot(h, wup_ref[:, cols])
        b = _dot(h, wup_ref[:, d_ff + j * FF_CHUNK:d_ff + (j + 1) * FF_CHUNK])
        acc = acc + _dot((jax.nn.silu(a) * b).astype(BF16), wdown_ref[cols, :])
    out_ref[0] = _rmsnorm(x + mod_ref[0, 5:6, :] * acc, gfin_ref[...])


def _ffn_call(x, mod, g_ffn, g_final, w_up, w_down):
    b, l, d = x.shape
    const = lambda shape: pl.BlockSpec(shape, lambda i, t: (0,) * len(shape),
                                       pipeline_mode=pl.Buffered(1))
    return pl.pallas_call(
        _ffn_kernel,
        grid=(b, l // TF),
        in_specs=[
            pl.BlockSpec((1, TF, d), lambda i, t: (i, t, 0)),
            pl.BlockSpec((1, MOD_ROWS, d), lambda i, t: (i, 0, 0)),
            const((1, d)),
            const((1, d)),
            const(w_up.shape),
            const(w_down.shape),
        ],
        out_specs=pl.BlockSpec((1, TF, d), lambda i, t: (i, t, 0)),
        out_shape=jax.ShapeDtypeStruct((b, l, d), F32),
        compiler_params=pltpu.CompilerParams(
            dimension_semantics=("arbitrary", "arbitrary"), vmem_limit_bytes=VMEM_LIMIT_BYTES),
        name="ffn",
    )(x, mod, g_ffn, g_final, w_up, w_down)


def kernel(x, c, ctx, c_ctx, w_mod, b_mod, g_mix, g_ffn, w_in, lb_gamma, g_norm_a, ln_v_g, ln_v_b,
           w_s, b_s, w_pa, w_pb, w_o, w_up, w_down, g_final):
    b, l, d = x.shape
    assert w_mod.shape[0] == 1 and lb_gamma.shape[0] == 2, "single-layer block only"
    assert l % TM == 0 and l % TF == 0 and TM % SUB == 0 and d % OUT_BLOCK == 0
    assert SUB % SGU_T == 0 and SGU_T % CHUNK == 0 and SUB // CHUNK <= MOD_ROWS
    assert (1 << CHUNK_SHIFT) == CHUNK and ctx.shape[1] % 8 == 0
    d_ff = w_down.shape[1]
    assert d_ff % FF_CHUNK == 0 and w_in.shape[2] == C_GA + 2 * d

    pad_rows = -(b + 1) % 8
    c_all = jnp.concatenate([c, c_ctx[None, :], jnp.zeros((pad_rows, d), F32)], axis=0)
    mod_all = _mod_call(c_all, w_mod[0], b_mod)
    pad_mod = lambda m: jnp.pad(m, [(0, 0)] * (m.ndim - 2) + [(0, MOD_ROWS - N_MOD), (0, 0)])
    mod = pad_mod(mod_all[:b].reshape(b, N_MOD, d))
    mc = pad_mod(mod_all[b].reshape(N_MOD, d))

    w_in_bf = w_in[0].astype(BF16)
    lbg = lb_gamma.reshape(2, 2 * KW)
    s0 = _ctx_call(ctx, mc, g_mix, lbg, w_in_bf)

    bs_full = jnp.repeat(jnp.transpose(b_s[0]), GW, axis=1)
    x1 = _mixer_call(x, mod, s0, g_mix, lbg, g_norm_a, ln_v_g, ln_v_b, bs_full,
                     w_s[0].astype(BF16), w_in_bf, w_pa[0].astype(BF16), w_pb[0].astype(BF16),
                     w_o[0].astype(BF16))

    return _ffn_call(x1, mod, g_ffn, g_final[None, :], w_up[0].astype(BF16),
                     w_down[0].astype(BF16))
```

```python
import functools

import jax
import jax.numpy as jnp
from jax import lax
from jax.experimental import pallas as pl
from jax.experimental.pallas import tpu as pltpu

F32 = jnp.float32
BF16 = jnp.bfloat16
EPS = 1e-6

HEADS = 4
DK = 128
DV = 128
KW = HEADS * DK
VW = HEADS * DV
CHUNK = 64
CHUNK_SHIFT = 6
GROUPS = 4
GW = 128
BW = GROUPS * GW
SGU_T = 128
N_MOD = 6
MOD_ROWS = 8

TM = 512
SUB = 256
TF = 512
OUT_BLOCK = 256
FF_CHUNK = 256

VMEM_LIMIT_BYTES = 60 * 1024 * 1024

C_FF, C_I, C_Q, C_OG, C_U, C_V, C_GA = (
    0, 2 * KW, 2 * KW + VW, 3 * KW + VW, 3 * KW + 2 * VW,
    3 * KW + 2 * VW + BW, 3 * KW + 2 * VW + 2 * BW)


def _dot(a, b):
    return jnp.dot(a, b, preferred_element_type=F32)


def _dot_nt(a, b):
    return lax.dot_general(a, b, (((1,), (1,)), ((), ())), preferred_element_type=F32)


def _dot_tn(a, b):
    return lax.dot_general(a, b, (((0,), (0,)), ((), ())), preferred_element_type=F32)


def _rmsnorm(x, g):
    return x * lax.rsqrt(jnp.mean(x * x, axis=-1, keepdims=True) + EPS) * g


def _modulated_norm(x, g, shift, scale):
    return _rmsnorm(x, g) * (1 + scale) + shift


def _lower_bound(lbg_ref):
    g0 = lbg_ref[0:1, :]
    g1 = lbg_ref[1:2, :]
    m = jnp.maximum(g0, g1)
    e0 = jnp.exp(g0 - m)
    e1 = jnp.exp(g1 - m)
    return e0 / (e0 + e1)


def _forget_gate(z, lb):
    f = lb + (1 - lb) * jax.nn.sigmoid(z)
    return jnp.log(f), 1 - f


def _masked_sum(mask_bf16, a):
    n = a.shape[1]
    hi = a.astype(BF16)
    lo = (a - hi.astype(F32)).astype(BF16)
    s = _dot(mask_bf16, jnp.concatenate([hi, lo], axis=1))
    return s[:, :n] + s[:, n:]


def _chunk_masks(t):
    r = lax.broadcasted_iota(jnp.int32, (t, t), 0)
    c = lax.broadcasted_iota(jnp.int32, (t, t), 1)
    same = lax.shift_right_logical(r, CHUNK_SHIFT) == lax.shift_right_logical(c, CHUNK_SHIFT)
    return same & (c <= r), same & (c >= r)


def _mod_kernel(c_ref, w_ref, b_ref, o_ref):
    a = jax.nn.silu(c_ref[...]).astype(BF16)
    o_ref[...] = _dot(a, w_ref[...].astype(BF16)) + b_ref[...]


def _mod_call(c_all, w_mod, b_mod):
    rows, d = c_all.shape
    n = w_mod.shape[1]
    return pl.pallas_call(
        _mod_kernel,
        grid=(n // d,),
        in_specs=[
            pl.BlockSpec((rows, d), lambda j: (0, 0)),
            pl.BlockSpec((d, d), lambda j: (0, j)),
            pl.BlockSpec((1, d), lambda j: (0, j)),
        ],
        out_specs=pl.BlockSpec((rows, d), lambda j: (0, j)),
        out_shape=jax.ShapeDtypeStruct((rows, n), F32),
        compiler_params=pltpu.CompilerParams(dimension_semantics=("arbitrary",)),
        name="mod",
    )(c_all, w_mod, b_mod)


def _ctx_kernel(ctx_ref, mc_ref, g_ref, lbg_ref, w_ref, s_ref):
    lc = ctx_ref.shape[1]
    hc = _modulated_norm(ctx_ref[0], g_ref[...], mc_ref[0:1, :], mc_ref[1:2, :]).astype(BF16)
    pc = _dot(hc, w_ref[...])
    logf, k = _forget_gate(pc[:, :2 * KW], _lower_bound(lbg_ref))
    v = pc[:, 2 * KW:].astype(BF16)
    r = lax.broadcasted_iota(jnp.int32, (lc, lc), 0)
    c = lax.broadcasted_iota(jnp.int32, (lc, lc), 1)
    after = (c > r).astype(BF16)
    before = (c < r).astype(BF16)
    w_f = (k[:, :KW] * jnp.exp(_masked_sum(after, logf[:, :KW]))).astype(BF16)
    w_b = (k[:, KW:] * jnp.exp(_masked_sum(before, logf[:, KW:]))).astype(BF16)
    for h in range(HEADS):
        vh = v[:, h * DV:(h + 1) * DV]
        s_ref[0, 0, h] = _dot_tn(vh, w_f[:, h * DK:(h + 1) * DK])
        s_ref[0, 1, h] = _dot_tn(vh, w_b[:, h * DK:(h + 1) * DK])


def _ctx_call(ctx, mc, g_mix, lbg, w_in):
    b, lc, d = ctx.shape
    ncol = 2 * KW + VW
    return pl.pallas_call(
        _ctx_kernel,
        grid=(b,),
        in_specs=[
            pl.BlockSpec((1, lc, d), lambda i: (i, 0, 0)),
            pl.BlockSpec((MOD_ROWS, d), lambda i: (0, 0)),
            pl.BlockSpec((1, d), lambda i: (0, 0)),
            pl.BlockSpec((2, 2 * KW), lambda i: (0, 0)),
            pl.BlockSpec((d, ncol), lambda i: (0, 0)),
        ],
        out_specs=pl.BlockSpec((1, 2, HEADS, DV, DK), lambda i: (i, 0, 0, 0, 0)),
        out_shape=jax.ShapeDtypeStruct((b, 2, HEADS, DV, DK), F32),
        compiler_params=pltpu.CompilerParams(
            dimension_semantics=("arbitrary",), vmem_limit_bytes=VMEM_LIMIT_BYTES),
        name="ctx_state",
    )(ctx, mc, g_mix, lbg, w_in)


def _mixer_kernel(x_ref, mod_ref, s0_ref, gmix_ref, lbg_ref, gna_ref, lng_ref, lnb_ref,
                  bs_ref, ws_ref, win_ref, wpa_ref, wpb_ref, wo_ref,
                  out_ref,
                  stf_ref, stb_ref, opart_ref, qdecb_ref, kdecb_ref, vb_ref, decayb_ref, hb_ref,
                  *, n_tiles):
    s = pl.program_id(1)
    n_chunks = SUB // CHUNK
    n_sub = TM // SUB
    chunks = [(j, c) for j in range(n_sub) for c in range(n_chunks)]
    head = lambda hh: slice(hh * DK, (hh + 1) * DK)

    @pl.when(s == 0)
    def _():
        stf_ref[...] = s0_ref[0, 0]
        stb_ref[...] = s0_ref[0, 1]

    def ascend():
        tile0 = pl.multiple_of(s * TM, TM)
        gain = gmix_ref[...] * (1 + mod_ref[0, 1:2, :])
        x = x_ref[0]
        h = (x * lax.rsqrt(jnp.mean(x * x, axis=-1, keepdims=True) + EPS) * gain
             + mod_ref[0, 0:1, :]).astype(BF16)
        hb_ref[pl.ds(tile0, TM), :] = h
        zf = _dot(h, win_ref[:, C_FF:C_I])
        q = _dot(h, win_ref[:, C_Q:C_OG])
        v = _dot(h, win_ref[:, C_I:C_Q]).astype(BF16)
        vb_ref[pl.ds(tile0, TM), :] = v
        logf, k = _forget_gate(zf, _lower_bound(lbg_ref))
        masks = _chunk_masks(SUB)
        mask_bf = (masks[0].astype(BF16), masks[1].astype(BF16))
        full = lambda rs: jnp.concatenate(
            [jnp.broadcast_to(r, (CHUNK, KW)) for r in rs], axis=0)
        probs = [(j, d) for j in range(n_sub) for d in range(2)]
        rows_of = lambda j: slice(j * SUB, (j + 1) * SUB)
        cols_of = lambda d: slice(d * KW, (d + 1) * KW)
        crow = lambda c: slice(c * CHUNK, (c + 1) * CHUNK)
        bc = {p: _masked_sum(mask_bf[p[1]], logf[rows_of(p[0]), cols_of(p[1])]) for p in probs}
        q_in, k_in, qdec, kdec, decay = {}, {}, {}, {}, {}
        for p in probs:
            j, d = p
            edge = CHUNK - 1 if d == 0 else 0
            blast = [bc[p][c * CHUNK + edge:c * CHUNK + edge + 1, :] for c in range(n_chunks)]
            mid = full([0.5 * r for r in blast])
            emid = full([jnp.exp(0.5 * r) for r in blast])
            qi = q[rows_of(j)] * jnp.exp(bc[p] - mid)
            ki = k[rows_of(j), cols_of(d)] * jnp.exp(mid - bc[p])
            qdec[p] = (qi * emid).astype(BF16)
            kdec[p] = (ki * emid).astype(BF16)
            q_in[p] = qi.astype(BF16)
            k_in[p] = ki.astype(BF16)
            decay[p] = [jnp.exp(r) for r in blast]
        sc = {}
        for p in probs:
            for hh in range(HEADS):
                sc[p, hh] = jnp.where(masks[p[1]],
                                      _dot_nt(q_in[p][:, head(hh)], k_in[p][:, head(hh)]),
                                      0.0).astype(BF16)
        d_st = {(j, c, hh): _dot_tn(v[j * SUB + c * CHUNK:j * SUB + (c + 1) * CHUNK,
                                      hh * DV:(hh + 1) * DV], kdec[j, 0][crow(c), head(hh)])
                for (j, c) in chunks for hh in range(HEADS)}
        intra = {}
        for p in probs:
            intra[p] = jnp.concatenate(
                [_dot(sc[p, hh], v[rows_of(p[0]), hh * DV:(hh + 1) * DV]) for hh in range(HEADS)],
                axis=1)
        st = [stf_ref[hh] for hh in range(HEADS)]
        entering = {}
        for (j, c) in chunks:
            for hh in range(HEADS):
                entering[j, c, hh] = jnp.transpose(st[hh]).astype(BF16)
                st[hh] = st[hh] * decay[j, 0][c][:, head(hh)] + d_st[j, c, hh]
        for hh in range(HEADS):
            stf_ref[hh] = st[hh]
        for j in range(n_sub):
            row0 = pl.multiple_of(s * TM + j * SUB, SUB)
            inter = jnp.concatenate(
                [jnp.concatenate([_dot(qdec[j, 0][crow(c), head(hh)], entering[j, c, hh])
                                  for hh in range(HEADS)], axis=1) for c in range(n_chunks)], axis=0)
            opart_ref[pl.ds(row0, SUB), :] = intra[j, 0] + intra[j, 1] + inter
            qdecb_ref[pl.ds(row0, SUB), :] = qdec[j, 1]
            kdecb_ref[pl.ds(row0, SUB), :] = kdec[j, 1]
            decayb_ref[s * n_sub + j] = jnp.concatenate(
                decay[j, 1] + [jnp.zeros((MOD_ROWS - n_chunks, KW), F32)], axis=0)

    def descend(t):
        row0 = pl.multiple_of(t * TM, TM)
        d = x_ref.shape[2]
        h = hb_ref[pl.ds(row0, TM), :]
        crows = lambda j, c: pl.ds(row0 + j * SUB + c * CHUNK, CHUNK)
        d_st = {(j, c, hh): _dot_tn(vb_ref[crows(j, c), hh * DV:(hh + 1) * DV],
                                    kdecb_ref[crows(j, c), head(hh)])
                for (j, c) in chunks for hh in range(HEADS)}
        vv = _dot(h, win_ref[:, C_V:C_GA])
        u = _dot(h, win_ref[:, C_U:C_V])
        og = _dot(h, win_ref[:, C_OG:C_U])

        vg = jax.nn.gelu(vv)
        mu = jnp.mean(vg, axis=-1, keepdims=True)
        var = jnp.mean(jnp.square(vg - mu), axis=-1, keepdims=True)
        vn = ((vg - mu) * lax.rsqrt(var + EPS) * lng_ref[...] + lnb_ref[...]).astype(BF16)
        mixed = []
        for jj in range(TM // SGU_T):
            rows = slice(jj * SGU_T, (jj + 1) * SGU_T)
            mixed.append(jnp.concatenate(
                [_dot(ws_ref[g], vn[rows, g * GW:(g + 1) * GW]) for g in range(GROUPS)],
                axis=1) + bs_ref[...])

        decay_b = [decayb_ref[t * n_sub + j] for j in range(n_sub)]
        st = [stb_ref[hh] for hh in range(HEADS)]
        entering = {}
        for (j, c) in reversed(chunks):
            for hh in range(HEADS):
                entering[j, c, hh] = jnp.transpose(st[hh]).astype(BF16)
                st[hh] = st[hh] * decay_b[j][c:c + 1, head(hh)] + d_st[j, c, hh]
        for hh in range(HEADS):
            stb_ref[hh] = st[hh]
        inter = [jnp.concatenate([_dot(qdecb_ref[crows(j, c), head(hh)], entering[j, c, hh])
                                  for hh in range(HEADS)], axis=1) for (j, c) in chunks]

        n_out = d // OUT_BLOCK
        cb = lambda jb: slice(jb * OUT_BLOCK, (jb + 1) * OUT_BLOCK)
        gates = [(_dot(h, win_ref[:, C_GA + jb * OUT_BLOCK:C_GA + (jb + 1) * OUT_BLOCK]),
                  _dot(h, win_ref[:, C_GA + d + jb * OUT_BLOCK:C_GA + d + (jb + 1) * OUT_BLOCK]))
                 for jb in range(n_out)]

        o_m = (jax.nn.gelu(u) * jnp.concatenate(mixed, axis=0)).astype(BF16)
        y_b = _dot(o_m, wpb_ref[...])
        o = opart_ref[pl.ds(row0, TM), :] + jnp.concatenate(inter, axis=0)
        o_a = jnp.concatenate(
            [_rmsnorm(o[:, hh * DV:(hh + 1) * DV], gna_ref[...]) for hh in range(HEADS)], axis=1)
        o_a = (o_a * jax.nn.silu(og)).astype(BF16)
        y_a = _dot(o_a, wpa_ref[...])

        mix = jnp.zeros((TM, d), F32)
        for jb, (ga, gb) in enumerate(gates):
            merged = (jax.nn.sigmoid(ga) * y_a[:, cb(jb)]
                      + jax.nn.sigmoid(gb) * y_b[:, cb(jb)]).astype(BF16)
            mix = mix + _dot(merged, wo_ref[cb(jb), :])
        out_ref[0] = x_ref[0] + mod_ref[0, 2:3, :] * mix

    @pl.when(s < n_tiles)
    def _():
        ascend()

    @pl.when(s >= n_tiles)
    def _():
        descend(2 * n_tiles - 1 - s)


def _mixer_call(x, mod, s0, g_mix, lbg, g_norm_a, ln_g, ln_b, bs_full, w_s, w_in, w_pa, w_pb, w_o):
    b, l, d = x.shape
    n_tiles = l // TM
    const = lambda shape: pl.BlockSpec(shape, lambda i, s: (0,) * len(shape),
                                       pipeline_mode=pl.Buffered(1))
    tile = lambda i, s: (i, jnp.where(s < n_tiles, s, 2 * n_tiles - 1 - s), 0)
    out_tile = lambda i, s: (i, jnp.where(s < n_tiles, n_tiles - 1, 2 * n_tiles - 1 - s), 0)
    return pl.pallas_call(
        functools.partial(_mixer_kernel, n_tiles=n_tiles),
        grid=(b, 2 * n_tiles),
        in_specs=[
            pl.BlockSpec((1, TM, d), tile),
            pl.BlockSpec((1, MOD_ROWS, d), lambda i, s: (i, 0, 0)),
            pl.BlockSpec((1, 2, HEADS, DV, DK), lambda i, s: (i, 0, 0, 0, 0)),
            const((1, d)),
            const((2, 2 * KW)),
            const((1, DV)),
            const((1, BW)),
            const((1, BW)),
            const((SGU_T, BW)),
            const((GROUPS, SGU_T, SGU_T)),
            const(w_in.shape),
            const(w_pa.shape),
            const(w_pb.shape),
            const(w_o.shape),
        ],
        out_specs=pl.BlockSpec((1, TM, d), out_tile),
        out_shape=jax.ShapeDtypeStruct((b, l, d), F32),
        scratch_shapes=[
            pltpu.VMEM((HEADS, DV, DK), F32),
            pltpu.VMEM((HEADS, DV, DK), F32),
            pltpu.VMEM((l, VW), F32),
            pltpu.VMEM((l, KW), BF16),
            pltpu.VMEM((l, KW), BF16),
            pltpu.VMEM((l, VW), BF16),
            pltpu.VMEM((l // SUB, MOD_ROWS, KW), F32),
            pltpu.VMEM((l, d), BF16),
        ],
        compiler_params=pltpu.CompilerParams(
            dimension_semantics=("arbitrary", "arbitrary"), vmem_limit_bytes=VMEM_LIMIT_BYTES),
        name="mixer",
    )(x, mod, s0, g_mix, lbg, g_norm_a, ln_g, ln_b, bs_full, w_s, w_in, w_pa, w_pb, w_o)


def _ffn_kernel(x_ref, mod_ref, gffn_ref, gfin_ref, wup_ref, wdown_ref, out_ref):
    d_ff = wdown_ref.shape[0]
    x = x_ref[0]
    h = _modulated_norm(x, gffn_ref[...], mod_ref[0, 3:4, :], mod_ref[0, 4:5, :]).astype(BF16)
    acc = jnp.zeros(x.shape, F32)
    for j in range(d_ff // FF_CHUNK):
        cols = slice(j * FF_CHUNK, (j + 1) * FF_CHUNK)
        a = _dot(h, wup_ref[:, cols])
        b = _dot(h, wup_ref[:, d_ff + j * FF_CHUNK:d_ff + (j + 1) * FF_CHUNK])
        acc = acc + _dot((jax.nn.silu(a) * b).astype(BF16), wdown_ref[cols, :])
    out_ref[0] = _rmsnorm(x + mod_ref[0, 5:6, :] * acc, gfin_ref[...])


def _ffn_call(x, mod, g_ffn, g_final, w_up, w_down):
    b, l, d = x.shape
    const = lambda shape: pl.BlockSpec(shape, lambda i, t: (0,) * len(shape),
                                       pipeline_mode=pl.Buffered(1))
    return pl.pallas_call(
        _ffn_kernel,
        grid=(b, l // TF),
        in_specs=[
            pl.BlockSpec((1, TF, d), lambda i, t: (i, t, 0)),
            pl.BlockSpec((1, MOD_ROWS, d), lambda i, t: (i, 0, 0)),
            const((1, d)),
            const((1, d)),
            const(w_up.shape),
            const(w_down.shape),
        ],
        out_specs=pl.BlockSpec((1, TF, d), lambda i, t: (i, t, 0)),
        out_shape=jax.ShapeDtypeStruct((b, l, d), F32),
        compiler_params=pltpu.CompilerParams(
            dimension_semantics=("arbitrary", "arbitrary"), vmem_limit_bytes=VMEM_LIMIT_BYTES),
        name="ffn",
    )(x, mod, g_ffn, g_final, w_up, w_down)


def kernel(x, c, ctx, c_ctx, w_mod, b_mod, g_mix, g_ffn, w_in, lb_gamma, g_norm_a, ln_v_g, ln_v_b,
           w_s, b_s, w_pa, w_pb, w_o, w_up, w_down, g_final):
    b, l, d = x.shape
    assert w_mod.shape[0] == 1 and lb_gamma.shape[0] == 2, "single-layer block only"
    assert l % TM == 0 and l % TF == 0 and TM % SUB == 0 and d % OUT_BLOCK == 0
    assert SUB % SGU_T == 0 and SGU_T % CHUNK == 0 and SUB // CHUNK <= MOD_ROWS
    assert (1 << CHUNK_SHIFT) == CHUNK and ctx.shape[1] % 8 == 0
    d_ff = w_down.shape[1]
    assert d_ff % FF_CHUNK == 0 and w_in.shape[2] == C_GA + 2 * d

    pad_rows = -(b + 1) % 8
    c_all = jnp.concatenate([c, c_ctx[None, :], jnp.zeros((pad_rows, d), F32)], axis=0)
    mod_all = _mod_call(c_all, w_mod[0], b_mod)
    pad_mod = lambda m: jnp.pad(m, [(0, 0)] * (m.ndim - 2) + [(0, MOD_ROWS - N_MOD), (0, 0)])
    mod = pad_mod(mod_all[:b].reshape(b, N_MOD, d))
    mc = pad_mod(mod_all[b].reshape(N_MOD, d))

    w_in_bf = w_in[0].astype(BF16)
    lbg = lb_gamma.reshape(2, 2 * KW)
    s0 = _ctx_call(ctx, mc, g_mix, lbg, w_in_bf)

    bs_full = jnp.repeat(jnp.transpose(b_s[0]), GW, axis=1)
    x1 = _mixer_call(x, mod, s0, g_mix, lbg, g_norm_a, ln_v_g, ln_v_b, bs_full,
                     w_s[0].astype(BF16), w_in_bf, w_pa[0].astype(BF16), w_pb[0].astype(BF16),
                     w_o[0].astype(BF16))

    return _ffn_call(x1, mod, g_ffn, g_final[None, :], w_up[0].astype(BF16),
                     w_down[0].astype(BF16))
```

```python
import functools

import jax
import jax.numpy as jnp
from jax import lax
from jax.experimental import pallas as pl
from jax.experimental.pallas import tpu as pltpu

F32 = jnp.float32
BF16 = jnp.bfloat16
EPS = 1e-6

HEADS = 4
DK = 128
DV = 128
KW = HEADS * DK
VW = HEADS * DV
CHUNK = 64
CHUNK_SHIFT = 6
GROUPS = 4
GW = 128
BW = GROUPS * GW
SGU_T = 128
N_MOD = 6
MOD_ROWS = 8

TM = 512
SUB = 64
TF = 512
OUT_BLOCK = 256
FF_CHUNK = 256

VMEM_LIMIT_BYTES = 60 * 1024 * 1024

C_FF, C_I, C_Q, C_OG, C_U, C_V, C_GA = (
    0, 2 * KW, 2 * KW + VW, 3 * KW + VW, 3 * KW + 2 * VW,
    3 * KW + 2 * VW + BW, 3 * KW + 2 * VW + 2 * BW)


def _dot(a, b):
    return jnp.dot(a, b, preferred_element_type=F32)


def _dot_nt(a, b):
    return lax.dot_general(a, b, (((1,), (1,)), ((), ())), preferred_element_type=F32)


def _dot_tn(a, b):
    return lax.dot_general(a, b, (((0,), (0,)), ((), ())), preferred_element_type=F32)


def _rmsnorm(x, g):
    return x * lax.rsqrt(jnp.mean(x * x, axis=-1, keepdims=True) + EPS) * g


def _modulated_norm(x, g, shift, scale):
    return _rmsnorm(x, g) * (1 + scale) + shift


def _lower_bound(lbg_ref):
    g0 = lbg_ref[0:1, :]
    g1 = lbg_ref[1:2, :]
    m = jnp.maximum(g0, g1)
    e0 = jnp.exp(g0 - m)
    e1 = jnp.exp(g1 - m)
    return e0 / (e0 + e1)


def _forget_gate(z, lb):
    f = lb + (1 - lb) * jax.nn.sigmoid(z)
    return jnp.log(f), 1 - f


def _masked_sum(mask_bf16, a):
    n = a.shape[1]
    hi = a.astype(BF16)
    lo = (a - hi.astype(F32)).astype(BF16)
    s = _dot(mask_bf16, jnp.concatenate([hi, lo], axis=1))
    return s[:, :n] + s[:, n:]


def _chunk_masks(t):
    r = lax.broadcasted_iota(jnp.int32, (t, t), 0)
    c = lax.broadcasted_iota(jnp.int32, (t, t), 1)
    same = lax.shift_right_logical(r, CHUNK_SHIFT) == lax.shift_right_logical(c, CHUNK_SHIFT)
    return same & (c <= r), same & (c >= r)


def _mod_kernel(c_ref, w_ref, b_ref, o_ref):
    a = jax.nn.silu(c_ref[...]).astype(BF16)
    o_ref[...] = _dot(a, w_ref[...].astype(BF16)) + b_ref[...]


def _mod_call(c_all, w_mod, b_mod):
    rows, d = c_all.shape
    n = w_mod.shape[1]
    return pl.pallas_call(
        _mod_kernel,
        grid=(n // d,),
        in_specs=[
            pl.BlockSpec((rows, d), lambda j: (0, 0)),
            pl.BlockSpec((d, d), lambda j: (0, j)),
            pl.BlockSpec((1, d), lambda j: (0, j)),
        ],
        out_specs=pl.BlockSpec((rows, d), lambda j: (0, j)),
        out_shape=jax.ShapeDtypeStruct((rows, n), F32),
        compiler_params=pltpu.CompilerParams(dimension_semantics=("arbitrary",)),
        name="mod",
    )(c_all, w_mod, b_mod)


def _ctx_kernel(ctx_ref, mc_ref, g_ref, lbg_ref, w_ref, s_ref):
    lc = ctx_ref.shape[1]
    hc = _modulated_norm(ctx_ref[0], g_ref[...], mc_ref[0:1, :], mc_ref[1:2, :]).astype(BF16)
    pc = _dot(hc, w_ref[...])
    logf, k = _forget_gate(pc[:, :2 * KW], _lower_bound(lbg_ref))
    v = pc[:, 2 * KW:].astype(BF16)
    r = lax.broadcasted_iota(jnp.int32, (lc, lc), 0)
    c = lax.broadcasted_iota(jnp.int32, (lc, lc), 1)
    after = (c > r).astype(BF16)
    before = (c < r).astype(BF16)
    w_f = (k[:, :KW] * jnp.exp(_masked_sum(after, logf[:, :KW]))).astype(BF16)
    w_b = (k[:, KW:] * jnp.exp(_masked_sum(before, logf[:, KW:]))).astype(BF16)
    for h in range(HEADS):
        vh = v[:, h * DV:(h + 1) * DV]
        s_ref[0, 0, h] = _dot_tn(vh, w_f[:, h * DK:(h + 1) * DK])
        s_ref[0, 1, h] = _dot_tn(vh, w_b[:, h * DK:(h + 1) * DK])


def _ctx_call(ctx, mc, g_mix, lbg, w_in):
    b, lc, d = ctx.shape
    ncol = 2 * KW + VW
    return pl.pallas_call(
        _ctx_kernel,
        grid=(b,),
        in_specs=[
            pl.BlockSpec((1, lc, d), lambda i: (i, 0, 0)),
            pl.BlockSpec((MOD_ROWS, d), lambda i: (0, 0)),
            pl.BlockSpec((1, d), lambda i: (0, 0)),
            pl.BlockSpec((2, 2 * KW), lambda i: (0, 0)),
            pl.BlockSpec((d, ncol), lambda i: (0, 0)),
        ],
        out_specs=pl.BlockSpec((1, 2, HEADS, DV, DK), lambda i: (i, 0, 0, 0, 0)),
        out_shape=jax.ShapeDtypeStruct((b, 2, HEADS, DV, DK), F32),
        compiler_params=pltpu.CompilerParams(
            dimension_semantics=("arbitrary",), vmem_limit_bytes=VMEM_LIMIT_BYTES),
        name="ctx_state",
    )(ctx, mc, g_mix, lbg, w_in)


def _mixer_kernel(x_ref, mod_ref, s0_ref, gmix_ref, lbg_ref, gna_ref, lng_ref, lnb_ref,
                  bs_ref, ws_ref, win_ref, wpa_ref, wpb_ref, wo_ref,
                  out_ref,
                  stf_ref, stb_ref, opart_ref, qdecb_ref, kdecb_ref, vb_ref, decayb_ref, hb_ref,
                  *, n_tiles):
    s = pl.program_id(1)
    n_chunks = SUB // CHUNK
    n_sub = TM // SUB
    chunks = [(j, c) for j in range(n_sub) for c in range(n_chunks)]
    head = lambda hh: slice(hh * DK, (hh + 1) * DK)

    @pl.when(s == 0)
    def _():
        stf_ref[...] = s0_ref[0, 0]
        stb_ref[...] = s0_ref[0, 1]

    def ascend():
        tile0 = pl.multiple_of(s * TM, TM)
        gain = gmix_ref[...] * (1 + mod_ref[0, 1:2, :])
        x = x_ref[0]
        h = (x * lax.rsqrt(jnp.mean(x * x, axis=-1, keepdims=True) + EPS) * gain
             + mod_ref[0, 0:1, :]).astype(BF16)
        hb_ref[pl.ds(tile0, TM), :] = h
        zf = _dot(h, win_ref[:, C_FF:C_I])
        q = _dot(h, win_ref[:, C_Q:C_OG])
        logf, k = _forget_gate(zf, _lower_bound(lbg_ref))
        masks = _chunk_masks(SUB)
        mask_bf = (masks[0].astype(BF16), masks[1].astype(BF16))
        full = lambda rs: jnp.concatenate(
            [jnp.broadcast_to(r, (CHUNK, KW)) for r in rs], axis=0)
        probs = [(j, d) for j in range(n_sub) for d in range(2)]
        rows_of = lambda j: slice(j * SUB, (j + 1) * SUB)
        cols_of = lambda d: slice(d * KW, (d + 1) * KW)
        crow = lambda c: slice(c * CHUNK, (c + 1) * CHUNK)
        bc = {p: _masked_sum(mask_bf[p[1]], logf[rows_of(p[0]), cols_of(p[1])]) for p in probs}
        q_in, k_in, qdec, kdec, decay = {}, {}, {}, {}, {}
        for p in probs:
            j, d = p
            edge = CHUNK - 1 if d == 0 else 0
            blast = [bc[p][c * CHUNK + edge:c * CHUNK + edge + 1, :] for c in range(n_chunks)]
            mid = full([0.5 * r for r in blast])
            emid = full([jnp.exp(0.5 * r) for r in blast])
            e_up = jnp.exp(bc[p] - mid)
            qi = q[rows_of(j)] * e_up
            ki = k[rows_of(j), cols_of(d)] / e_up
            qdec[p] = (qi * emid).astype(BF16)
            kdec[p] = (ki * emid).astype(BF16)
            q_in[p] = qi.astype(BF16)
            k_in[p] = ki.astype(BF16)
            decay[p] = [jnp.exp(r) for r in blast]
        v = _dot(h, win_ref[:, C_I:C_Q]).astype(BF16)
        vb_ref[pl.ds(tile0, TM), :] = v
        sc = {}
        for p in probs:
            for hh in range(HEADS):
                sc[p, hh] = jnp.where(masks[p[1]],
                                      _dot_nt(q_in[p][:, head(hh)], k_in[p][:, head(hh)]),
                                      0.0).astype(BF16)
        d_st = {(j, c, hh): _dot_tn(v[j * SUB + c * CHUNK:j * SUB + (c + 1) * CHUNK,
                                      hh * DV:(hh + 1) * DV], kdec[j, 0][crow(c), head(hh)])
                for (j, c) in chunks for hh in range(HEADS)}
        intra = {}
        for p in probs:
            intra[p] = jnp.concatenate(
                [_dot(sc[p, hh], v[rows_of(p[0]), hh * DV:(hh + 1) * DV]) for hh in range(HEADS)],
                axis=1)
        st = [stf_ref[hh] for hh in range(HEADS)]
        entering = {}
        for (j, c) in chunks:
            for hh in range(HEADS):
                entering[j, c, hh] = jnp.transpose(st[hh]).astype(BF16)
                st[hh] = st[hh] * decay[j, 0][c][:, head(hh)] + d_st[j, c, hh]
        for hh in range(HEADS):
            stf_ref[hh] = st[hh]
        for j in range(n_sub):
            row0 = pl.multiple_of(s * TM + j * SUB, SUB)
            inter = jnp.concatenate(
                [jnp.concatenate([_dot(qdec[j, 0][crow(c), head(hh)], entering[j, c, hh])
                                  for hh in range(HEADS)], axis=1) for c in range(n_chunks)], axis=0)
            opart_ref[pl.ds(row0, SUB), :] = intra[j, 0] + intra[j, 1] + inter
            qdecb_ref[pl.ds(row0, SUB), :] = qdec[j, 1]
            kdecb_ref[pl.ds(row0, SUB), :] = kdec[j, 1]
            decayb_ref[s * n_sub + j] = jnp.concatenate(
                decay[j, 1] + [jnp.zeros((MOD_ROWS - n_chunks, KW), F32)], axis=0)

    def descend(t):
        row0 = pl.multiple_of(t * TM, TM)
        d = x_ref.shape[2]
        h = hb_ref[pl.ds(row0, TM), :]
        crows = lambda j, c: pl.ds(row0 + j * SUB + c * CHUNK, CHUNK)
        d_st = {(j, c, hh): _dot_tn(vb_ref[crows(j, c), hh * DV:(hh + 1) * DV],
                                    kdecb_ref[crows(j, c), head(hh)])
                for (j, c) in chunks for hh in range(HEADS)}
        vv = _dot(h, win_ref[:, C_V:C_GA])
        u = _dot(h, win_ref[:, C_U:C_V])
        og = _dot(h, win_ref[:, C_OG:C_U])

        vg = jax.nn.gelu(vv)
        mu = jnp.mean(vg, axis=-1, keepdims=True)
        var = jnp.mean(jnp.square(vg - mu), axis=-1, keepdims=True)
        vn = ((vg - mu) * lax.rsqrt(var + EPS) * lng_ref[...] + lnb_ref[...]).astype(BF16)
        mixed = []
        for jj in range(TM // SGU_T):
            rows = slice(jj * SGU_T, (jj + 1) * SGU_T)
            mixed.append(jnp.concatenate(
                [_dot(ws_ref[g], vn[rows, g * GW:(g + 1) * GW]) for g in range(GROUPS)],
                axis=1) + bs_ref[...])

        decay_b = [decayb_ref[t * n_sub + j] for j in range(n_sub)]
        st = [stb_ref[hh] for hh in range(HEADS)]
        entering = {}
        for (j, c) in reversed(chunks):
            for hh in range(HEADS):
                entering[j, c, hh] = jnp.transpose(st[hh]).astype(BF16)
                st[hh] = st[hh] * decay_b[j][c:c + 1, head(hh)] + d_st[j, c, hh]
        for hh in range(HEADS):
            stb_ref[hh] = st[hh]
        inter = [jnp.concatenate([_dot(qdecb_ref[crows(j, c), head(hh)], entering[j, c, hh])
                                  for hh in range(HEADS)], axis=1) for (j, c) in chunks]

        n_out = d // OUT_BLOCK
        cb = lambda jb: slice(jb * OUT_BLOCK, (jb + 1) * OUT_BLOCK)
        gates = [(_dot(h, win_ref[:, C_GA + jb * OUT_BLOCK:C_GA + (jb + 1) * OUT_BLOCK]),
                  _dot(h, win_ref[:, C_GA + d + jb * OUT_BLOCK:C_GA + d + (jb + 1) * OUT_BLOCK]))
                 for jb in range(n_out)]

        o_m = (jax.nn.gelu(u) * jnp.concatenate(mixed, axis=0)).astype(BF16)
        y_b = _dot(o_m, wpb_ref[...])
        o = opart_ref[pl.ds(row0, TM), :] + jnp.concatenate(inter, axis=0)
        o_a = jnp.concatenate(
            [_rmsnorm(o[:, hh * DV:(hh + 1) * DV], gna_ref[...]) for hh in range(HEADS)], axis=1)
        o_a = (o_a * jax.nn.silu(og)).astype(BF16)
        y_a = _dot(o_a, wpa_ref[...])

        mix = jnp.zeros((TM, d), F32)
        for jb, (ga, gb) in enumerate(gates):
            merged = (jax.nn.sigmoid(ga) * y_a[:, cb(jb)]
                      + jax.nn.sigmoid(gb) * y_b[:, cb(jb)]).astype(BF16)
            mix = mix + _dot(merged, wo_ref[cb(jb), :])
        out_ref[0] = x_ref[0] + mod_ref[0, 2:3, :] * mix

    @pl.when(s < n_tiles)
    def _():
        ascend()

    @pl.when(s >= n_tiles)
    def _():
        descend(2 * n_tiles - 1 - s)


def _mixer_call(x, mod, s0, g_mix, lbg, g_norm_a, ln_g, ln_b, bs_full, w_s, w_in, w_pa, w_pb, w_o):
    b, l, d = x.shape
    n_tiles = l // TM
    const = lambda shape: pl.BlockSpec(shape, lambda i, s: (0,) * len(shape),
                                       pipeline_mode=pl.Buffered(1))
    tile = lambda i, s: (i, jnp.where(s < n_tiles, s, 2 * n_tiles - 1 - s), 0)
    out_tile = lambda i, s: (i, jnp.where(s < n_tiles, n_tiles - 1, 2 * n_tiles - 1 - s), 0)
    return pl.pallas_call(
        functools.partial(_mixer_kernel, n_tiles=n_tiles),
        grid=(b, 2 * n_tiles),
        in_specs=[
            pl.BlockSpec((1, TM, d), tile),
            pl.BlockSpec((1, MOD_ROWS, d), lambda i, s: (i, 0, 0)),
            pl.BlockSpec((1, 2, HEADS, DV, DK), lambda i, s: (i, 0, 0, 0, 0)),
            const((1, d)),
            const((2, 2 * KW)),
            const((1, DV)),
            const((1, BW)),
            const((1, BW)),
            const((SGU_T, BW)),
            const((GROUPS, SGU_T, SGU_T)),
            const(w_in.shape),
            const(w_pa.shape),
            const(w_pb.shape),
            const(w_o.shape),
        ],
        out_specs=pl.BlockSpec((1, TM, d), out_tile),
        out_shape=jax.ShapeDtypeStruct((b, l, d), F32),
        scratch_shapes=[
            pltpu.VMEM((HEADS, DV, DK), F32),
            pltpu.VMEM((HEADS, DV, DK), F32),
            pltpu.VMEM((l, VW), F32),
            pltpu.VMEM((l, KW), BF16),
            pltpu.VMEM((l, KW), BF16),
            pltpu.VMEM((l, VW), BF16),
            pltpu.VMEM((l // SUB, MOD_ROWS, KW), F32),
            pltpu.VMEM((l, d), BF16),
        ],
        compiler_params=pltpu.CompilerParams(
            dimension_semantics=("arbitrary", "arbitrary"), vmem_limit_bytes=VMEM_LIMIT_BYTES),
        name="mixer",
    )(x, mod, s0, g_mix, lbg, g_norm_a, ln_g, ln_b, bs_full, w_s, w_in, w_pa, w_pb, w_o)


def _ffn_kernel(x_ref, mod_ref, gffn_ref, gfin_ref, wup_ref, wdown_ref, out_ref):
    d_ff = wdown_ref.shape[0]
    x = x_ref[0]
    h = _modulated_norm(x, gffn_ref[...], mod_ref[0, 3:4, :], mod_ref[0, 4:5, :]).astype(BF16)
    acc = jnp.zeros(x.shape, F32)
    for j in range(d_ff // FF_CHUNK):
        cols = slice(j * FF_CHUNK, (j + 1) * FF_CHUNK)
        a = _dot(h, wup_ref[:, cols])
        b = _dot(h, wup_ref[:, d_ff + j * FF_CHUNK:d_ff + (j + 1) * FF_CHUNK])
        acc = acc + _dot((jax.nn.silu(a) * b).astype(BF16), wdown_ref[cols, :])
    out_ref[0] = _rmsnorm(x + mod_ref[0, 5:6, :] * acc, gfin_ref[...])


def _ffn_call(x, mod, g_ffn, g_final, w_up, w_down):
    b, l, d = x.shape
    const = lambda shape: pl.BlockSpec(shape, lambda i, t: (0,) * len(shape),
                                       pipeline_mode=pl.Buffered(1))
    return pl.pallas_call(
        _ffn_kernel,
        grid=(b, l // TF),
        in_specs=[
            pl.BlockSpec((1, TF, d), lambda i, t: (i, t, 0)),
            pl.BlockSpec((1, MOD_ROWS, d), lambda i, t: (i, 0, 0)),
            const((1, d)),
            const((1, d)),
            const(w_up.shape),
            const(w_down.shape),
        ],
        out_specs=pl.BlockSpec((1, TF, d), lambda i, t: (i, t, 0)),
        out_shape=jax.ShapeDtypeStruct((b, l, d), F32),
        compiler_params=pltpu.CompilerParams(
            dimension_semantics=("arbitrary", "arbitrary"), vmem_limit_bytes=VMEM_LIMIT_BYTES),
        name="ffn",
    )(x, mod, g_ffn, g_final, w_up, w_down)


def kernel(x, c, ctx, c_ctx, w_mod, b_mod, g_mix, g_ffn, w_in, lb_gamma, g_norm_a, ln_v_g, ln_v_b,
           w_s, b_s, w_pa, w_pb, w_o, w_up, w_down, g_final):
    b, l, d = x.shape
    assert w_mod.shape[0] == 1 and lb_gamma.shape[0] == 2, "single-layer block only"
    assert l % TM == 0 and l % TF == 0 and TM % SUB == 0 and d % OUT_BLOCK == 0
    assert TM % SGU_T == 0 and SUB % CHUNK == 0 and SUB // CHUNK <= MOD_ROWS
    assert (1 << CHUNK_SHIFT) == CHUNK and ctx.shape[1] % 8 == 0
    d_ff = w_down.shape[1]
    assert d_ff % FF_CHUNK == 0 and w_in.shape[2] == C_GA + 2 * d

    pad_rows = -(b + 1) % 8
    c_all = jnp.concatenate([c, c_ctx[None, :], jnp.zeros((pad_rows, d), F32)], axis=0)
    mod_all = _mod_call(c_all, w_mod[0], b_mod)
    pad_mod = lambda m: jnp.pad(m, [(0, 0)] * (m.ndim - 2) + [(0, MOD_ROWS - N_MOD), (0, 0)])
    mod = pad_mod(mod_all[:b].reshape(b, N_MOD, d))
    mc = pad_mod(mod_all[b].reshape(N_MOD, d))

    w_in_bf = w_in[0].astype(BF16)
    lbg = lb_gamma.reshape(2, 2 * KW)
    s0 = _ctx_call(ctx, mc, g_mix, lbg, w_in_bf)

    bs_full = jnp.repeat(jnp.transpose(b_s[0]), GW, axis=1)
    x1 = _mixer_call(x, mod, s0, g_mix, lbg, g_norm_a, ln_v_g, ln_v_b, bs_full,
                     w_s[0].astype(BF16), w_in_bf, w_pa[0].astype(BF16), w_pb[0].astype(BF16),
                     w_o[0].astype(BF16))

    return _ffn_call(x1, mod, g_ffn, g_final[None, :], w_up[0].astype(BF16),
                     w_down[0].astype(BF16))
```
